```python
import jax, jax.numpy as jnp
from jax import lax
import numpy as np

D_MODEL = 1024
BATCH = 8
SEQ = 4096
DEPTH = 4
DEC_BATCH = 16
DEC_SEQ = 32
PAST_LEN = 4096

CHUNK = 64
EPS = 1e-6
D_FF = 2816
PLE_DIM = 256
N_BRANCH = 4
BRANCH_WIDTH = D_MODEL // 2
POOL_WINDOWS = (2, 4, 8, 16)
POOL_GROUPS = len(POOL_WINDOWS)
POOL_GROUP_DIM = BRANCH_WIDTH // POOL_GROUPS
POOL_HIST = max(POOL_WINDOWS) - 1
SB_HEAD_DIM = 64
SB_HEADS = BRANCH_WIDTH // SB_HEAD_DIM
SB_WIDTH = SB_HEADS * SB_HEAD_DIM
SB_BLOCK = 128
GMLP_CHUNK = 128
GMLP_GROUPS = 4
GMLP_GROUP_DIM = BRANCH_WIDTH // GMLP_GROUPS
GLA_HEADS = 4
GLA_DK = BRANCH_WIDTH // (2 * GLA_HEADS)
GLA_DV = BRANCH_WIDTH // GLA_HEADS
GLA_GATE_RANK = 16
GLA_GATE_NORMALIZER = 16.0
IN_SPLITS = (BRANCH_WIDTH,
             SB_WIDTH, SB_WIDTH, SB_WIDTH,
             BRANCH_WIDTH, BRANCH_WIDTH,
             GLA_HEADS * GLA_DK, GLA_HEADS * GLA_DK,
             GLA_HEADS * GLA_DV, GLA_GATE_RANK,
             GLA_HEADS * GLA_DV,
             N_BRANCH * D_MODEL)
IN_COLS = sum(IN_SPLITS)

kernel_name = "hybrid_streaming_encoder_step"


def rms_norm(x, g):
    xf = x.astype(jnp.float32)
    y = xf * lax.rsqrt(jnp.mean(xf * xf, axis=-1, keepdims=True) + EPS)
    return (y * g.astype(jnp.float32)).astype(x.dtype)


def swiglu(x, w1, w3, w2):
    return (jax.nn.silu(x @ w1) * (x @ w3)) @ w2


def split_columns(z):
    idx = np.cumsum(IN_SPLITS)[:-1].tolist()
    return jnp.split(z, idx, axis=-1)


def pool_mixer(xp, hist, pos0, pool_w, pool_scale):
    B, L, W = xp.shape
    ext_in = jnp.concatenate([hist, xp], axis=1)
    ext = ext_in.astype(jnp.float32)
    cs = jnp.concatenate([jnp.zeros((B, 1, W), jnp.float32), jnp.cumsum(ext, axis=1)], axis=1)
    end = cs[:, POOL_HIST + 1:]
    pos = pos0 + jnp.arange(L)
    means = []
    for g, w in enumerate(POOL_WINDOWS):
        sl = slice(g * POOL_GROUP_DIM, (g + 1) * POOL_GROUP_DIM)
        start = cs[:, POOL_HIST + 1 - w:POOL_HIST + 1 - w + L, sl]
        cnt = jnp.minimum(w, pos + 1).astype(jnp.float32)[None, :, None]
        means.append((end[..., sl] - start) / cnt)
    d = jnp.concatenate(means, axis=-1) - ext[:, POOL_HIST:]
    d = d.reshape(B, L, POOL_GROUPS, POOL_GROUP_DIM)
    y = jnp.einsum('blgc,gcd->blgd', d, pool_w.astype(jnp.float32)).reshape(B, L, W)
    y = (y * pool_scale.astype(jnp.float32)).astype(xp.dtype)
    return y, ext_in[:, -POOL_HIST:]


def sb_block(qb, qpos, k, v, kpos):
    z = jnp.einsum('bqhd,bkhd->bhqk', qb, k) * (SB_HEAD_DIM ** -0.5)
    mask = kpos[None, :] < qpos[:, None]
    l = jnp.where(mask, jax.nn.log_sigmoid(-z), 0.0)
    after = lax.cumsum(l, axis=3, reverse=True) - l
    a = jnp.where(mask, jnp.exp(jax.nn.log_sigmoid(z) + after), 0.0)
    return jnp.einsum('bhqk,bkhd->bqhd', a, v)


def stick_breaking_attention(q, k_all, v_all):
    B, L, H, d = q.shape
    Lk = k_all.shape[1]
    kpos = jnp.arange(Lk)
    qpos = (Lk - L) + jnp.arange(L)
    qf, kf, vf = q.astype(jnp.float32), k_all.astype(jnp.float32), v_all.astype(jnp.float32)
    if L <= SB_BLOCK:
        o = sb_block(qf, qpos, kf, vf, kpos)
    else:
        nb = L // SB_BLOCK
        qb = qf.reshape(B, nb, SB_BLOCK, H, d).transpose(1, 0, 2, 3, 4)
        pb = qpos.reshape(nb, SB_BLOCK)
        o = lax.map(lambda a: sb_block(a[0], a[1], kf, vf, kpos), (qb, pb))
        o = o.transpose(1, 0, 2, 3, 4).reshape(B, L, H, d)
    return o.astype(q.dtype)


def spatial_gating(u, v, ws, bs):
    B, L, W = u.shape
    c = min(L, GMLP_CHUNK)
    n = L // c
    idx = jnp.arange(c)
    mask = (idx[None, :] // CHUNK) <= (idx[:, None] // CHUNK)
    wm = jnp.where(mask[None], ws[:, :c, :c], 0.0)
    vc = v.reshape(B, n, c, GMLP_GROUPS, GMLP_GROUP_DIM)
    s = jnp.einsum('gij,bnjgc->bnigc', wm, vc) + bs[:, :c].T[None, None, :, :, None]
    return u * s.reshape(B, L, W)


def gla_chunk(S, q, k, v, g):
    c = q.shape[1]
    b = jnp.cumsum(g, axis=1)
    inter = jnp.einsum('bthk,bhkv->bthv', q * jnp.exp(b), S)
    mask = jnp.tril(jnp.ones((c, c), bool))[None, :, :, None, None]
    diff = b[:, :, None] - b[:, None, :]
    decay = jnp.where(mask, jnp.exp(jnp.where(mask, diff, 0.0)), 0.0)
    att = jnp.einsum('bthk,bshk,btshk->bhts', q, k, decay)
    intra = jnp.einsum('bhts,bshv->bthv', att, v)
    b_last = b[:, -1]
    S_new = jnp.exp(b_last)[..., None] * S + jnp.einsum('bshk,bshv->bhkv', k * jnp.exp(b_last[:, None] - b), v)
    return S_new, inter + intra


def gla_recurrence(q, k, v, g, s0):
    B, L, H, _ = q.shape
    qf, kf, vf, gf = (a.astype(jnp.float32) for a in (q, k, v, g))
    S0 = s0.astype(jnp.float32)
    c = min(L, CHUNK)
    n = L // c
    if n == 1:
        S, o = gla_chunk(S0, qf, kf, vf, gf)
    else:
        def to_chunks(a):
            return a.reshape(B, n, c, *a.shape[2:]).swapaxes(0, 1)
        S, o = lax.scan(lambda s, xs: gla_chunk(s, *xs), S0,
                        (to_chunks(qf), to_chunks(kf), to_chunks(vf), to_chunks(gf)))
        o = o.swapaxes(0, 1).reshape(B, L, H, GLA_DV)
    return S.astype(s0.dtype), o.astype(v.dtype)


def token_mixing(h, pool_hist, k_past, v_past, s0, w_in, pool_w, pool_scale, sb_qn, sb_kn,
                 gmlp_ws, gmlp_b, gla_wa2, gla_ba, gla_on, w_branch, w_out):
    B, L, _ = h.shape
    pos0 = k_past.shape[1]
    (xp, q, k, v, gu, gv, lq, lk, lv, la, lr, gates) = split_columns(h @ w_in)
    y_pool, pool_new = pool_mixer(xp, pool_hist, pos0, pool_w, pool_scale)
    q = rms_norm(q.reshape(B, L, SB_HEADS, SB_HEAD_DIM), sb_qn)
    k = rms_norm(k.reshape(B, L, SB_HEADS, SB_HEAD_DIM), sb_kn)
    v = v.reshape(B, L, SB_HEADS, SB_HEAD_DIM)
    y_sb = stick_breaking_attention(q, jnp.concatenate([k_past, k], axis=1),
                                    jnp.concatenate([v_past, v], axis=1)).reshape(B, L, SB_WIDTH)
    gu, gv = jax.nn.gelu(gu), jax.nn.gelu(gv)
    y_gmlp = spatial_gating(gu, gv, gmlp_ws, gmlp_b)
    log_a = jax.nn.log_sigmoid((la @ gla_wa2 + gla_ba).astype(jnp.float32)) / GLA_GATE_NORMALIZER
    s_new, o = gla_recurrence(lq.reshape(B, L, GLA_HEADS, GLA_DK) * (GLA_DK ** -0.5),
                              lk.reshape(B, L, GLA_HEADS, GLA_DK),
                              lv.reshape(B, L, GLA_HEADS, GLA_DV),
                              log_a.reshape(B, L, GLA_HEADS, GLA_DK), s0)
    y_gla = rms_norm(o, gla_on).reshape(B, L, BRANCH_WIDTH) * jax.nn.silu(lr)
    gates = jax.nn.sigmoid(gates.reshape(B, L, N_BRANCH, D_MODEL))
    merged = gates[:, :, 0] * (y_pool @ w_branch[0])
    merged = merged + gates[:, :, 1] * (y_sb @ w_branch[1])
    merged = merged + gates[:, :, 2] * (y_gmlp @ w_branch[2])
    merged = merged + gates[:, :, 3] * (y_gla @ w_branch[3])
    return merged @ w_out, pool_new, k, v, s_new, gv


def layer(x, p_i, pool_hist, k_past, v_past, s0, lw):
    (n1, f1a, f1b, f1c, nm, w_in, pool_w, pool_scale, qn, kn, ws, bs, wa2, ba, on,
     w_branch, w_out, n2, f2a, f2b, f2c, npl, wpg, wpp) = lw
    x = x + 0.5 * swiglu(rms_norm(x, n1), f1a, f1b, f1c)
    m, pool_new, k, v, s_new, gv = token_mixing(rms_norm(x, nm), pool_hist, k_past, v_past, s0, w_in,
                                                pool_w, pool_scale, qn, kn, ws, bs, wa2, ba, on,
                                                w_branch, w_out)
    x = x + m
    x = x + 0.5 * swiglu(rms_norm(x, n2), f2a, f2b, f2c)
    x = x + jax.nn.sigmoid(rms_norm(x, npl) @ wpg) * (p_i @ wpp)
    return x, pool_new, k, v, s_new, gv


def setup_inputs(seed: int = 0) -> dict:
    key = jax.random.key(seed)
    ks = iter(list(jax.random.split(key, 40)))

    def nrm(shape, scale):
        return jax.random.normal(next(ks), shape, jnp.float32) * scale

    def gain(shape):
        return 1.0 + 0.05 * jax.random.normal(next(ks), shape, jnp.float32)

    return {
        'x_prompt': nrm((BATCH, SEQ, D_MODEL), 1.0),
        'x_sample': nrm((DEC_BATCH, DEC_SEQ, D_MODEL), 1.0),
        'p_prompt': nrm((DEPTH, BATCH, SEQ, PLE_DIM), 1.0),
        'p_sample': nrm((DEPTH, DEC_BATCH, DEC_SEQ, PLE_DIM), 1.0),
        'cache_sb_k': nrm((DEPTH, DEC_BATCH, PAST_LEN, SB_HEADS, SB_HEAD_DIM), 1.0),
        'cache_sb_v': nrm((DEPTH, DEC_BATCH, PAST_LEN, SB_HEADS, SB_HEAD_DIM), 1.0),
        'state_pool': nrm((DEPTH, DEC_BATCH, POOL_HIST, BRANCH_WIDTH), 1.0),
        'state_gla': nrm((DEPTH, DEC_BATCH, GLA_HEADS, GLA_DK, GLA_DV), 0.3),
        'norm_ffn1': gain((DEPTH, D_MODEL)),
        'ffn1_w1': nrm((DEPTH, D_MODEL, D_FF), D_MODEL ** -0.5),
        'ffn1_w3': nrm((DEPTH, D_MODEL, D_FF), D_MODEL ** -0.5),
        'ffn1_w2': nrm((DEPTH, D_FF, D_MODEL), D_FF ** -0.5),
        'norm_mix': gain((DEPTH, D_MODEL)),
        'w_in': nrm((DEPTH, D_MODEL, IN_COLS), D_MODEL ** -0.5),
        'pool_w': nrm((DEPTH, POOL_GROUPS, POOL_GROUP_DIM, POOL_GROUP_DIM), POOL_GROUP_DIM ** -0.5),
        'pool_scale': gain((DEPTH, BRANCH_WIDTH)),
        'sb_q_norm': gain((DEPTH, SB_HEAD_DIM)),
        'sb_k_norm': gain((DEPTH, SB_HEAD_DIM)),
        'gmlp_ws': nrm((DEPTH, GMLP_GROUPS, GMLP_CHUNK, GMLP_CHUNK), GMLP_CHUNK ** -0.5),
        'gmlp_b': gain((DEPTH, GMLP_GROUPS, GMLP_CHUNK)),
        'gla_wa2': nrm((DEPTH, GLA_GATE_RANK, GLA_HEADS * GLA_DK), GLA_GATE_RANK ** -0.5),
        'gla_ba': nrm((DEPTH, GLA_HEADS * GLA_DK), 0.01),
        'gla_out_norm': gain((DEPTH, GLA_DV)),
        'w_branch': nrm((DEPTH, N_BRANCH, BRANCH_WIDTH, D_MODEL), BRANCH_WIDTH ** -0.5),
        'w_out': nrm((DEPTH, D_MODEL, D_MODEL), D_MODEL ** -0.5),
        'norm_ffn2': gain((DEPTH, D_MODEL)),
        'ffn2_w1': nrm((DEPTH, D_MODEL, D_FF), D_MODEL ** -0.5),
        'ffn2_w3': nrm((DEPTH, D_MODEL, D_FF), D_MODEL ** -0.5),
        'ffn2_w2': nrm((DEPTH, D_FF, D_MODEL), D_FF ** -0.5),
        'norm_ple': gain((DEPTH, D_MODEL)),
        'ple_w_gate': nrm((DEPTH, D_MODEL, D_MODEL), D_MODEL ** -0.5),
        'ple_w_proj': nrm((DEPTH, PLE_DIM, D_MODEL), PLE_DIM ** -0.5),
    }


def reference(x_prompt, x_sample, p_prompt, p_sample, cache_sb_k, cache_sb_v, state_pool, state_gla,
              norm_ffn1, ffn1_w1, ffn1_w3, ffn1_w2, norm_mix, w_in, pool_w, pool_scale,
              sb_q_norm, sb_k_norm, gmlp_ws, gmlp_b, gla_wa2, gla_ba, gla_out_norm,
              w_branch, w_out, norm_ffn2, ffn2_w1, ffn2_w3, ffn2_w2, norm_ple, ple_w_gate, ple_w_proj):
    lw_all = (norm_ffn1, ffn1_w1, ffn1_w3, ffn1_w2, norm_mix, w_in, pool_w, pool_scale,
              sb_q_norm, sb_k_norm, gmlp_ws, gmlp_b, gla_wa2, gla_ba, gla_out_norm,
              w_branch, w_out, norm_ffn2, ffn2_w1, ffn2_w3, ffn2_w2, norm_ple, ple_w_gate, ple_w_proj)
    B, S, _ = x_prompt.shape
    dt = x_prompt.dtype
    pool_hist0 = jnp.zeros((B, POOL_HIST, BRANCH_WIDTH), dt)
    kv_past0 = jnp.zeros((B, 0, SB_HEADS, SB_HEAD_DIM), dt)
    gla0 = jnp.zeros((B, GLA_HEADS, GLA_DK, GLA_DV), dt)
    yp, ys = x_prompt, x_sample
    pk, pv, ppool, pgla = [], [], [], []
    sk, sv, spool, sgla, sgmlp = [], [], [], [], []
    for i in range(DEPTH):
        lw = tuple(w[i] for w in lw_all)
        yp, pool_p, k_p, v_p, s_p, _ = layer(yp, p_prompt[i], pool_hist0, kv_past0, kv_past0, gla0, lw)
        ys, pool_s, k_s, v_s, s_s, gv_s = layer(ys, p_sample[i], state_pool[i], cache_sb_k[i],
                                                cache_sb_v[i], state_gla[i], lw)
        pk.append(k_p); pv.append(v_p); ppool.append(pool_p); pgla.append(s_p)
        sk.append(k_s); sv.append(v_s); spool.append(pool_s); sgla.append(s_s); sgmlp.append(gv_s)
    return (yp, ys,
            jnp.stack(pk), jnp.stack(pv), jnp.stack(ppool), jnp.stack(pgla),
            jnp.stack(sk), jnp.stack(sv), jnp.stack(spool), jnp.stack(sgla), jnp.stack(sgmlp))
```

```python
import functools

import jax
import jax.numpy as jnp
from jax import lax
from jax.experimental import pallas as pl
from jax.experimental.pallas import tpu as pltpu

F32 = jnp.float32
BF16 = jnp.bfloat16

EPS = 1e-6
CHUNK = 64
N_BRANCH = 4
POOL_WINDOWS = (2, 4, 8, 16)
POOL_HIST = max(POOL_WINDOWS) - 1
POOL_CARRY = POOL_HIST + 1
SB_HEAD_DIM = 64
GMLP_CHUNK = 128
GMLP_GROUPS = 4
GLA_HEADS = 4
GLA_DK = 64
GLA_DV = 128
GLA_GATE_RANK = 16
GLA_GATE_NORMALIZER = 16.0
LANES = 128
ROW_TILE = 512
FF_CHUNK = 256
SB_BLOCK = 256
SB_PAST_TILE = 512
GLA_TILE = 256
VMEM_LIMIT_BYTES = 48 * 1024 * 1024

_NN = (((1,), (0,)), ((), ()))
_NT = (((1,), (1,)), ((), ()))
_TN = (((0,), (0,)), ((), ()))


def _dot(a, b, dims=_NN):
    return lax.dot_general(a, b, dims, preferred_element_type=F32)


def _rms(x, g):
    return x * lax.rsqrt(jnp.mean(x * x, axis=-1, keepdims=True) + EPS) * g


def _gelu_tanh(x):
    cdf = 0.5 * (1.0 + jnp.tanh(0.7978845608028654 * (x + 0.044715 * (x * x * x))))
    return x * cdf


def _silu(x):
    return x * jax.nn.sigmoid(x)


def _softplus_tail(x):
    return jnp.log1p(jnp.exp(-jnp.abs(x)))


def _split_bf16(x):
    hi = x.astype(BF16)
    lo = (x - hi.astype(F32)).astype(BF16)
    return hi, lo


def _block_id(idx, size):
    assert size & (size - 1) == 0
    return lax.shift_right_logical(idx, size.bit_length() - 1)


def _resident(shape):
    zeros = (0,) * len(shape)
    return pl.BlockSpec(shape, lambda *_: zeros, pipeline_mode=pl.Buffered(1))


def _params():
    return pltpu.CompilerParams(vmem_limit_bytes=VMEM_LIMIT_BYTES)


def _ffn_body(*refs, n_chunks, with_ple):
    if with_ple:
        x_ref, g_ref, w1_ref, w3_ref, w2_ref, p_ref, gp_ref, wpg_ref, wpp_ref, o_ref, h_ref, acc_ref = refs
    else:
        x_ref, g_ref, w1_ref, w3_ref, w2_ref, o_ref, h_ref, acc_ref = refs
    h_ref[...] = _rms(x_ref[...], g_ref[...]).astype(BF16)
    acc_ref[...] = jnp.zeros_like(acc_ref)

    def chunk(j, carry):
        h = h_ref[...]
        a = _dot(h, w1_ref[j])
        b = _dot(h, w3_ref[j])
        acc_ref[...] += _dot((_silu(a) * b).astype(BF16), w2_ref[j])
        return carry

    lax.fori_loop(0, n_chunks, chunk, 0)
    y = x_ref[...] + 0.5 * acc_ref[...]
    if with_ple:
        hp = _rms(y, gp_ref[...]).astype(BF16)
        gate = jax.nn.sigmoid(_dot(hp, wpg_ref[...]))
        y = y + gate * _dot(p_ref[...].astype(BF16), wpp_ref[...])
    o_ref[...] = y


def _ffn(x, g, w1, w3, w2, ple=None):
    T, D = x.shape
    tm = min(ROW_TILE, T)
    n_chunks = w1.shape[0]
    row = lambda i: (i, 0)
    in_specs = [pl.BlockSpec((tm, D), row), _resident(g.shape), _resident(w1.shape), _resident(w3.shape),
                _resident(w2.shape)]
    args = [x, g, w1, w3, w2]
    if ple is not None:
        p, gp, wpg, wpp = ple
        in_specs += [pl.BlockSpec((tm, p.shape[1]), row), _resident(gp.shape), _resident(wpg.shape),
                     _resident(wpp.shape)]
        args += [p, gp, wpg, wpp]
    return pl.pallas_call(
        functools.partial(_ffn_body, n_chunks=n_chunks, with_ple=ple is not None),
        out_shape=jax.ShapeDtypeStruct((T, D), F32),
        grid=(T // tm,),
        in_specs=in_specs,
        out_specs=pl.BlockSpec((tm, D), row),
        scratch_shapes=[pltpu.VMEM((tm, D), BF16), pltpu.VMEM((tm, D), F32)],
        compiler_params=_params(),
        name="ffn_ple" if ple is not None else "ffn",
    )(*args)


def _inproj_body(x_ref, g_ref, wm_ref, wla_ref, wlr_ref, bd_ref, qn_ref, kn_ref, wa2_ref, ba_ref,
                 xp_ref, q_ref, k_ref, kb_ref, v_ref, vb_ref, u_ref, gv_ref, lqk_ref, lv_ref, la_ref, lr_ref,
                 h_ref):
    h_ref[...] = _rms(x_ref[...], g_ref[...]).astype(BF16)

    def col(i):
        return _dot(h_ref[...], wm_ref[i])

    def head_norm(z, gain):
        ms = _dot((z * z).astype(BF16), bd_ref[...])
        return z * lax.rsqrt(ms + EPS) * gain

    xp_ref[...] = col(0)
    q = head_norm(col(1), qn_ref[...])
    q_ref[...] = (q * (SB_HEAD_DIM ** -0.5)).astype(BF16)
    k = head_norm(col(2), kn_ref[...])
    k_ref[...] = k
    kb_ref[...] = k.astype(BF16)
    v = col(3)
    v_ref[...] = v
    vb_ref[...] = v.astype(BF16)
    u_ref[...] = _gelu_tanh(col(4)).astype(BF16)
    gv_ref[...] = _gelu_tanh(col(5)).astype(gv_ref.dtype)
    lqk_ref[...] = col(6)
    lv_ref[...] = col(7)
    la = _dot(h_ref[...], wla_ref[...])
    xg = _dot(la.astype(BF16), wa2_ref[...]) + ba_ref[...]
    la_ref[...] = (jnp.minimum(xg, 0.0) - _softplus_tail(xg)) / GLA_GATE_NORMALIZER
    lr_ref[...] = _silu(_dot(h_ref[...], wlr_ref[...])).astype(BF16)


def _inproj(x, g, wm, wla, wlr, bd, qn, kn, wa2, ba, gv_dtype):
    T, D = x.shape
    tm = min(ROW_TILE, T)
    W = wm.shape[2]
    row = lambda i: (i, 0)
    consts = [g, wm, wla, wlr, bd, qn, kn, wa2, ba]
    out_dtypes = [F32, BF16, F32, BF16, F32, BF16, BF16, gv_dtype, F32, F32, F32, BF16]
    out_widths = [W] * 10 + [wa2.shape[1], W]
    return pl.pallas_call(
        _inproj_body,
        out_shape=[jax.ShapeDtypeStruct((T, w), d) for w, d in zip(out_widths, out_dtypes)],
        grid=(T // tm,),
        in_specs=[pl.BlockSpec((tm, D), row)] + [_resident(c.shape) for c in consts],
        out_specs=[pl.BlockSpec((tm, w), row) for w in out_widths],
        scratch_shapes=[pltpu.VMEM((tm, D), BF16)],
        compiler_params=_params(),
        name="inproj",
    )(x, *consts)


def _pool_body(xp_ref, hist_ref, pw_ref, ps_ref, y_ref, prev_ref, *, pos0, tl):
    t = pl.program_id(1)

    @pl.when(t == 0)
    def _():
        prev_ref[...] = hist_ref[0]

    x = xp_ref[0]
    ext = jnp.concatenate([prev_ref[...], x], axis=0)
    pos = pos0 + t * tl + lax.broadcasted_iota(jnp.int32, (tl, 1), 0)
    for g, w in enumerate(POOL_WINDOWS):
        cols = slice(g * LANES, (g + 1) * LANES)
        s = ext[:, cols]
        span = 1
        while span < w:
            s = s + pltpu.roll(s, span, 0)
            span *= 2
        cnt = jnp.minimum(w, pos + 1).astype(F32)
        d = s[POOL_CARRY:, :] / cnt - x[:, cols]
        y = _dot(d.astype(BF16), pw_ref[g]) * ps_ref[:, cols]
        y_ref[0, :, cols] = y.astype(y_ref.dtype)
    prev_ref[...] = x[tl - POOL_CARRY:, :]


def _pool(xp, hist, pw, ps, pos0):
    B, L, W = xp.shape
    tl = min(ROW_TILE, L)
    return pl.pallas_call(
        functools.partial(_pool_body, pos0=pos0, tl=tl),
        out_shape=jax.ShapeDtypeStruct((B, L, W), BF16),
        grid=(B, L // tl),
        in_specs=[pl.BlockSpec((1, tl, W), lambda b, t: (b, t, 0)),
                  pl.BlockSpec((1, POOL_CARRY, W), lambda b, t: (b, 0, 0)),
                  _resident(pw.shape), _resident(ps.shape)],
        out_specs=pl.BlockSpec((1, tl, W), lambda b, t: (b, t, 0)),
        scratch_shapes=[pltpu.VMEM((POOL_CARRY, W), F32)],
        compiler_params=_params(),
        name="pool",
    )(xp, hist, pw, ps)


def _later_key_matrix(n):
    j = lax.broadcasted_iota(jnp.int32, (n, n), 0)
    s = lax.broadcasted_iota(jnp.int32, (n, n), 1)
    return jnp.where(j > s, 1.0, 0.0).astype(BF16)


def _sb_block(qh, kj, vj, later, mask, run, acc):
    z = _dot(qh, kj, _NT)
    tail = _softplus_tail(z)
    log_rest = -(jnp.maximum(z, 0.0) + tail)
    if mask is not None:
        log_rest = jnp.where(mask, log_rest, 0.0)
    hi, lo = _split_bf16(log_rest)
    after = _dot(hi, later) + _dot(lo, later)
    a = jnp.exp(jnp.minimum(z, 0.0) - tail + after + run)
    if mask is not None:
        a = jnp.where(mask, a, 0.0)
    acc = acc + _dot(a.astype(BF16), vj)
    run = run + after[:, 0:1] + log_rest[:, 0:1]
    return run, acc


def _sb_prompt_body(q_ref, k_ref, v_ref, o_ref, *, blk):
    i = pl.program_id(1)
    later = _later_key_matrix(blk)
    qi = lax.broadcasted_iota(jnp.int32, (blk, blk), 0)
    ki = lax.broadcasted_iota(jnp.int32, (blk, blk), 1)
    diag_mask = ki < qi
    heads_per_slab = LANES // SB_HEAD_DIM
    for slab in range(q_ref.shape[2] // LANES):
        cols = slice(slab * LANES, (slab + 1) * LANES)
        q2 = q_ref[0, :, cols]
        outs = []
        for hh in range(heads_per_slab):
            hc = slice(hh * SB_HEAD_DIM, (hh + 1) * SB_HEAD_DIM)
            qh = q2[:, hc]

            def visit(j, mask, run, acc, hc=hc, qh=qh, cols=cols):
                rows = pl.ds(pl.multiple_of(j * blk, blk), blk)
                kj = k_ref[0, rows, cols][:, hc]
                vj = v_ref[0, rows, cols][:, hc]
                return _sb_block(qh, kj, vj, later, mask, run, acc)

            run, acc = visit(i, diag_mask, jnp.zeros((blk, 1), F32), jnp.zeros((blk, SB_HEAD_DIM), F32))
            run, acc = lax.fori_loop(0, i, lambda it, c: visit(i - 1 - it, None, *c), (run, acc))
            outs.append(acc)
        o_ref[0, :, cols] = jnp.concatenate(outs, axis=-1).astype(o_ref.dtype)


def _sb_prompt(q, k, v):
    B, L, W = q.shape
    blk = min(SB_BLOCK, L)
    return pl.pallas_call(
        functools.partial(_sb_prompt_body, blk=blk),
        out_shape=jax.ShapeDtypeStruct((B, L, W), BF16),
        grid=(B, L // blk),
        in_specs=[pl.BlockSpec((1, blk, W), lambda b, i: (b, i, 0)),
                  pl.BlockSpec((1, L, W), lambda b, i: (b, 0, 0)),
                  pl.BlockSpec((1, L, W), lambda b, i: (b, 0, 0))],
        out_specs=pl.BlockSpec((1, blk, W), lambda b, i: (b, i, 0)),
        compiler_params=_params(),
        name="sb_prompt",
    )(q, k, v)


def _sb_sample_body(q_ref, kn_ref, vn_ref, kp_ref, vp_ref, o_ref, run_ref, acc_ref, *, lq, blk, n_sub):
    j = pl.program_id(1)
    n_heads = q_ref.shape[2] // SB_HEAD_DIM
    heads_per_slab = LANES // SB_HEAD_DIM

    def head_cols(h):
        slab, hh = divmod(h, heads_per_slab)
        return slice(slab * LANES, (slab + 1) * LANES), slice(hh * SB_HEAD_DIM, (hh + 1) * SB_HEAD_DIM)

    @pl.when(j == 0)
    def _():
        later = _later_key_matrix(lq)
        qi = lax.broadcasted_iota(jnp.int32, (lq, lq), 0)
        ki = lax.broadcasted_iota(jnp.int32, (lq, lq), 1)
        mask = ki < qi
        for h in range(n_heads):
            cols, hc = head_cols(h)
            run, acc = _sb_block(q_ref[0, :, cols][:, hc], kn_ref[0, :, cols][:, hc], vn_ref[0, :, cols][:, hc],
                                 later, mask, jnp.zeros((lq, 1), F32), jnp.zeros((lq, SB_HEAD_DIM), F32))
            run_ref[h] = run
            acc_ref[h] = acc

    later = _later_key_matrix(blk)
    for h in range(n_heads):
        cols, hc = head_cols(h)
        qh = q_ref[0, :, cols][:, hc]
        run, acc = run_ref[h], acc_ref[h]
        for sub in reversed(range(n_sub)):
            rows = slice(sub * blk, (sub + 1) * blk)
            kj = kp_ref[0, rows, cols][:, hc].astype(BF16)
            vj = vp_ref[0, rows, cols][:, hc].astype(BF16)
            run, acc = _sb_block(qh, kj, vj, later, None, run, acc)
        run_ref[h] = run
        acc_ref[h] = acc

    @pl.when(j == pl.num_programs(1) - 1)
    def _():
        for slab in range(n_heads // heads_per_slab):
            parts = [acc_ref[slab * heads_per_slab + hh] for hh in range(heads_per_slab)]
            o_ref[0, :, slab * LANES:(slab + 1) * LANES] = jnp.concatenate(parts, axis=-1).astype(o_ref.dtype)


def _sb_sample(q, k_new, v_new, k_past, v_past):
    B, L, W = q.shape
    P = k_past.shape[1]
    tile = min(SB_PAST_TILE, P)
    blk = min(SB_BLOCK, tile)
    n_tiles = P // tile
    new = pl.BlockSpec((1, L, W), lambda b, j: (b, 0, 0))
    past = pl.BlockSpec((1, tile, W), lambda b, j: (b, n_tiles - 1 - j, 0))
    n_heads = W // SB_HEAD_DIM
    return pl.pallas_call(
        functools.partial(_sb_sample_body, lq=L, blk=blk, n_sub=tile // blk),
        out_shape=jax.ShapeDtypeStruct((B, L, W), BF16),
        grid=(B, n_tiles),
        in_specs=[new, new, new, past, past],
        out_specs=new,
        scratch_shapes=[pltpu.VMEM((n_heads, L, 1), F32), pltpu.VMEM((n_heads, L, SB_HEAD_DIM), F32)],
        compiler_params=_params(),
        name="sb_sample",
    )(q, k_new, v_new, k_past, v_past)


def _gmlp_body(u_ref, v_ref, ws_ref, bst_ref, y_ref, *, c, tl):
    i = lax.broadcasted_iota(jnp.int32, (c, c), 0)
    j = lax.broadcasted_iota(jnp.int32, (c, c), 1)
    causal = _block_id(j, CHUNK) <= _block_id(i, CHUNK)
    for g in range(GMLP_GROUPS):
        cols = slice(g * LANES, (g + 1) * LANES)
        wm = jnp.where(causal, ws_ref[g], 0.0).astype(BF16)
        bias = bst_ref[:, g:g + 1]
        for r in range(tl // c):
            rows = slice(r * c, (r + 1) * c)
            s = _dot(wm, v_ref[0, rows, cols].astype(BF16)) + bias
            y_ref[0, rows, cols] = (u_ref[0, rows, cols].astype(F32) * s).astype(y_ref.dtype)


def _gmlp(u, v, ws, bst):
    B, L, W = u.shape
    c = ws.shape[1]
    tl = min(ROW_TILE, L)
    blk = pl.BlockSpec((1, tl, W), lambda b, t: (b, t, 0))
    return pl.pallas_call(
        functools.partial(_gmlp_body, c=c, tl=tl),
        out_shape=jax.ShapeDtypeStruct((B, L, W), BF16),
        grid=(B, L // tl),
        in_specs=[blk, blk, _resident(ws.shape), _resident(bst.shape)],
        out_specs=blk,
        compiler_params=_params(),
        name="gmlp",
    )(u, v, ws, bst)


def _gla_body(lqk_ref, lv_ref, la_ref, lr_ref, s0_ref, on_ref, y_ref, sout_ref, st_ref, *, c, tl):
    t = pl.program_id(1)

    @pl.when(t == 0)
    def _():
        st_ref[...] = s0_ref[0]

    r = lax.broadcasted_iota(jnp.int32, (tl, tl), 0)
    s = lax.broadcasted_iota(jnp.int32, (tl, tl), 1)
    within = jnp.where((s <= r) & (_block_id(s, c) == _block_id(r, c)), 1.0, 0.0).astype(BF16)
    g_hi, g_lo = _split_bf16(la_ref[0])
    b_all = _dot(within, g_hi) + _dot(within, g_lo)

    ti = lax.broadcasted_iota(jnp.int32, (c, c), 0)
    si = lax.broadcasted_iota(jnp.int32, (c, c), 1)
    tril = si <= ti
    hk = GLA_HEADS * GLA_DK
    for ch in range(tl // c):
        rows = slice(ch * c, (ch + 1) * c)
        lqk = lqk_ref[0, rows, :]
        lv = lv_ref[0, rows, :]
        b_rows = b_all[rows, :]
        outs = []
        for h in range(GLA_HEADS):
            kc = slice(h * GLA_DK, (h + 1) * GLA_DK)
            vc = slice(h * GLA_DV, (h + 1) * GLA_DV)
            q = lqk[:, kc] * (GLA_DK ** -0.5)
            k = lqk[:, hk + h * GLA_DK: hk + (h + 1) * GLA_DK]
            v = lv[:, vc].astype(BF16)
            b = b_rows[:, kc]
            b_mid = b[c // 2 - 1:c // 2, :]
            b_last = b[c - 1:c, :]
            qe = (q * jnp.exp(b - b_mid)).astype(BF16)
            ke = (k * jnp.exp(b_mid - b)).astype(BF16)
            att = jnp.where(tril, _dot(qe, ke, _NT), 0.0)
            st = st_ref[h]
            o = _dot((q * jnp.exp(b)).astype(BF16), st.astype(BF16), _NT) + _dot(att.astype(BF16), v)
            kd = (k * jnp.exp(b_last - b)).astype(BF16)
            st_ref[h] = st * jnp.exp(b_last) + _dot(v, kd, _TN)
            outs.append(_rms(o, on_ref[...]))
        y = jnp.concatenate(outs, axis=-1) * lr_ref[0, rows, :].astype(F32)
        y_ref[0, rows, :] = y.astype(y_ref.dtype)

    @pl.when(t == pl.num_programs(1) - 1)
    def _():
        sout_ref[0] = st_ref[...]


def _gla(lqk, lv, la, lr, s0t, on):
    B, L, _ = lqk.shape
    c = min(CHUNK, L)
    tl = min(GLA_TILE, L)
    blk = lambda a: pl.BlockSpec((1, tl, a.shape[2]), lambda b, t: (b, t, 0))
    state = pl.BlockSpec((1,) + s0t.shape[1:], lambda b, t: (b, 0, 0, 0))
    return pl.pallas_call(
        functools.partial(_gla_body, c=c, tl=tl),
        out_shape=[jax.ShapeDtypeStruct(lv.shape, BF16), jax.ShapeDtypeStruct(s0t.shape, F32)],
        grid=(B, L // tl),
        in_specs=[blk(lqk), blk(lv), blk(la), blk(lr), state, _resident(on.shape)],
        out_specs=[blk(lv), state],
        scratch_shapes=[pltpu.VMEM(s0t.shape[1:], F32)],
        compiler_params=_params(),
        name="gla",
    )(lqk, lv, la, lr, s0t, on)


def _merge_body(x_ref, g_ref, wg_ref, y0_ref, y1_ref, y2_ref, y3_ref, wb_ref, wo_ref, o_ref, h_ref):
    h_ref[...] = _rms(x_ref[...], g_ref[...]).astype(BF16)
    merged = None
    for b, y_ref in enumerate((y0_ref, y1_ref, y2_ref, y3_ref)):
        term = jax.nn.sigmoid(_dot(h_ref[...], wg_ref[b])) * _dot(y_ref[...], wb_ref[b])
        merged = term if merged is None else merged + term
    o_ref[...] = x_ref[...] + _dot(merged.astype(BF16), wo_ref[...])


def _merge(x, g, wg, ys, wb, wo):
    T, D = x.shape
    tm = min(ROW_TILE, T)
    row = lambda i: (i, 0)
    return pl.pallas_call(
        _merge_body,
        out_shape=jax.ShapeDtypeStruct((T, D), F32),
        grid=(T // tm,),
        in_specs=[pl.BlockSpec((tm, D), row), _resident(g.shape), _resident(wg.shape)]
        + [pl.BlockSpec((tm, y.shape[1]), row) for y in ys] + [_resident(wb.shape), _resident(wo.shape)],
        out_specs=pl.BlockSpec((tm, D), row),
        scratch_shapes=[pltpu.VMEM((tm, D), BF16)],
        compiler_params=_params(),
        name="merge",
    )(x, g, wg, *ys, wb, wo)


def _layer(x, p, pool_hist, k_past, v_past, s0, lw, want_gv):
    B, L, D = x.shape
    T = B * L
    W = lw["pool_scale"].shape[-1]
    x2 = x.reshape(T, D)
    x2 = _ffn(x2, lw["norm_ffn1"], lw["ffn1_w1"], lw["ffn1_w3"], lw["ffn1_w2"])

    (xp, q, k, kb, v, vb, u, gv, lqk, lv, la, lr) = _inproj(
        x2, lw["norm_mix"], lw["w_main"], lw["w_la"], lw["w_lr"], lw["head_avg"], lw["sb_q_norm"],
        lw["sb_k_norm"], lw["gla_wa2"], lw["gla_ba"], F32 if want_gv else BF16)
    seq = lambda a: a.reshape(B, L, a.shape[-1])

    pos0 = 0 if k_past is None else k_past.shape[1]
    hist = jnp.pad(pool_hist, ((0, 0), (POOL_CARRY - POOL_HIST, 0), (0, 0)))
    y_pool = _pool(seq(xp), hist, lw["pool_w"], lw["pool_scale"], pos0)
    pool_new = seq(xp)[:, L - POOL_HIST:, :]

    if k_past is None:
        y_sb = _sb_prompt(seq(q), seq(kb), seq(vb))
    else:
        P = k_past.shape[1]
        y_sb = _sb_sample(seq(q), seq(kb), seq(vb), k_past.reshape(B, P, W), v_past.reshape(B, P, W))

    c = min(L, GMLP_CHUNK)
    y_gmlp = _gmlp(seq(u), seq(gv), lw["gmlp_ws"][:, :c, :c], lw["gmlp_bt"][:c, :])

    y_gla, st = _gla(seq(lqk), seq(lv), seq(la), seq(lr), jnp.swapaxes(s0, -1, -2), lw["gla_out_norm"])
    s_new = jnp.swapaxes(st, -1, -2)

    ys = [y.reshape(T, W) for y in (y_pool, y_sb, y_gmlp, y_gla)]
    x2 = _merge(x2, lw["norm_mix"], lw["w_gate"], ys, lw["w_branch"], lw["w_out"])
    x2 = _ffn(x2, lw["norm_ffn2"], lw["ffn2_w1"], lw["ffn2_w3"], lw["ffn2_w2"],
              ple=(p.reshape(T, p.shape[-1]), lw["norm_ple"], lw["ple_w_gate"], lw["ple_w_proj"]))

    heads = W // SB_HEAD_DIM
    k_out = k.reshape(B, L, heads, SB_HEAD_DIM)
    v_out = v.reshape(B, L, heads, SB_HEAD_DIM)
    return x2.reshape(B, L, D), pool_new, k_out, v_out, s_new, (seq(gv) if want_gv else None)


def _prepare_weights(norm_ffn1, ffn1_w1, ffn1_w3, ffn1_w2, norm_mix, w_in, pool_w, pool_scale,
                     sb_q_norm, sb_k_norm, gmlp_ws, gmlp_b, gla_wa2, gla_ba, gla_out_norm,
                     w_branch, w_out, norm_ffn2, ffn2_w1, ffn2_w3, ffn2_w2, norm_ple, ple_w_gate, ple_w_proj):
    depth, D, d_ff = ffn1_w1.shape
    W = pool_scale.shape[-1]
    heads = W // SB_HEAD_DIM
    n_ff = d_ff // FF_CHUNK

    def up(w):
        return w.astype(BF16).reshape(depth, D, n_ff, FF_CHUNK).transpose(0, 2, 1, 3)

    def down(w):
        return w.astype(BF16).reshape(depth, n_ff, FF_CHUNK, D)

    row = lambda a: a.reshape(depth, 1, a.shape[-1])
    n_main = 8 * W
    o_la = n_main
    o_lr = o_la + GLA_GATE_RANK
    o_gate = o_lr + W
    w_in16 = w_in.astype(BF16)
    w_main = w_in16[:, :, :n_main].reshape(depth, D, 8, W).transpose(0, 2, 1, 3)
    w_la = jnp.pad(w_in16[:, :, o_la:o_lr], ((0, 0), (0, 0), (0, LANES - GLA_GATE_RANK)))
    w_lr = w_in16[:, :, o_lr:o_gate]
    w_gate = w_in16[:, :, o_gate:].reshape(depth, D, N_BRANCH, D).transpose(0, 2, 1, 3)
    wa2 = jnp.pad(gla_wa2.astype(BF16), ((0, 0), (0, LANES - GLA_GATE_RANK), (0, 0)))
    head_id = jnp.arange(W) // SB_HEAD_DIM
    head_avg = jnp.where(head_id[:, None] == head_id[None, :], 1.0 / SB_HEAD_DIM, 0.0).astype(BF16)
    stacked = dict(
        norm_ffn1=row(norm_ffn1), ffn1_w1=up(ffn1_w1), ffn1_w3=up(ffn1_w3), ffn1_w2=down(ffn1_w2),
        norm_mix=row(norm_mix), w_main=w_main, w_la=w_la, w_lr=w_lr, w_gate=w_gate,
        pool_w=pool_w.astype(BF16), pool_scale=row(pool_scale),
        sb_q_norm=row(jnp.tile(sb_q_norm, (1, heads))), sb_k_norm=row(jnp.tile(sb_k_norm, (1, heads))),
        gmlp_ws=gmlp_ws, gmlp_bt=jnp.swapaxes(gmlp_b, -1, -2),
        gla_wa2=wa2, gla_ba=row(gla_ba), gla_out_norm=row(gla_out_norm),
        w_branch=w_branch.astype(BF16), w_out=w_out.astype(BF16),
        norm_ffn2=row(norm_ffn2), ffn2_w1=up(ffn2_w1), ffn2_w3=up(ffn2_w3), ffn2_w2=down(ffn2_w2),
        norm_ple=row(norm_ple), ple_w_gate=ple_w_gate.astype(BF16), ple_w_proj=ple_w_proj.astype(BF16),
    )
    return [dict({name: a[i] for name, a in stacked.items()}, head_avg=head_avg) for i in range(depth)]


def kernel(x_prompt, x_sample, p_prompt, p_sample, cache_sb_k, cache_sb_v, state_pool, state_gla, norm_ffn1, ffn1_w1, ffn1_w3, ffn1_w2, norm_mix, w_in, pool_w, pool_scale, sb_q_norm, sb_k_norm, gmlp_ws, gmlp_b, gla_wa2, gla_ba, gla_out_norm, w_branch, w_out, norm_ffn2, ffn2_w1, ffn2_w3, ffn2_w2, norm_ple, ple_w_gate, ple_w_proj):
    layers = _prepare_weights(norm_ffn1, ffn1_w1, ffn1_w3, ffn1_w2, norm_mix, w_in, pool_w, pool_scale,
                              sb_q_norm, sb_k_norm, gmlp_ws, gmlp_b, gla_wa2, gla_ba, gla_out_norm,
                              w_branch, w_out, norm_ffn2, ffn2_w1, ffn2_w3, ffn2_w2, norm_ple, ple_w_gate,
                              ple_w_proj)
    B = x_prompt.shape[0]
    W = pool_scale.shape[-1]
    pool_hist0 = jnp.zeros((B, POOL_HIST, W), F32)
    gla0 = jnp.zeros((B,) + state_gla.shape[2:], F32)
    yp, ys = x_prompt, x_sample
    prompt_out = [[] for _ in range(4)]
    sample_out = [[] for _ in range(5)]
    for i, lw in enumerate(layers):
        yp, pool_p, k_p, v_p, s_p, _ = _layer(yp, p_prompt[i], pool_hist0, None, None, gla0, lw, False)
        ys, pool_s, k_s, v_s, s_s, gv_s = _layer(ys, p_sample[i], state_pool[i], cache_sb_k[i], cache_sb_v[i],
                                                 state_gla[i], lw, True)
        for acc, a in zip(prompt_out, (k_p, v_p, pool_p, s_p)):
            acc.append(a)
        for acc, a in zip(sample_out, (k_s, v_s, pool_s, s_s, gv_s)):
            acc.append(a)
    return (yp, ys, *(jnp.stack(a) for a in prompt_out), *(jnp.stack(a) for a in sample_out))
```
